```python
import math
import jax, jax.numpy as jnp
from jax import lax
import numpy as np

D_MODEL = 1024
BATCH = 16
SEQ = 4096
DEPTH = 1
DEC_BATCH = 4
DEC_SEQ = 4096
PAST_LEN = 128

GRID_W = 64
PLE_DIM = 256
D_RNN = 1024
RNN_BLOCKS = 8
RNN_BLOCK_W = D_RNN // RNN_BLOCKS
RNN_CONV_W = 4
LRU_C = 8.0
N_HEADS = 8
HEAD_DIM = 128
D_ATT = N_HEADS * HEAD_DIM
NA_ROWS = 8
NA_COLS = 16
NA_QCOLS = 16
NA_KCOLS = 32
D_FF = 3072
FFN_CONV_W = 3
EPS = 1e-6
NEG_INF = -1e30
SPLITS = (D_RNN, D_RNN, D_ATT, D_ATT, D_ATT, D_MODEL, D_MODEL)
D_IN = sum(SPLITS)

kernel_name = "hybrid_rglru_natten_encoder"


def rmsnorm(x, g):
    x32 = x.astype(jnp.float32)
    y = x32 * lax.rsqrt(jnp.mean(x32 * x32, axis=-1, keepdims=True) + EPS) * g.astype(jnp.float32)
    return y.astype(x.dtype)


def depthwise_conv(x, w, b, left, right):
    C = x.shape[-1]
    y = lax.conv_general_dilated(x, w.astype(x.dtype)[:, None, :], window_strides=(1,),
                                 padding=[(left, right)], dimension_numbers=('NWC', 'WIO', 'NWC'),
                                 feature_group_count=C)
    return y + b.astype(x.dtype)


def _block_diag(x, w, b):
    B, S, _ = x.shape
    y = jnp.einsum('bsnc,ncd->bsnd', x.reshape(B, S, RNN_BLOCKS, RNN_BLOCK_W), w)
    return y.reshape(B, S, D_RNN) + b


def _lin_combine(e1, e2):
    a1, b1 = e1
    a2, b2 = e2
    return a1 * a2, a2 * b1 + b2


def rglru_bidir(xc, w_rgate, b_rgate, w_igate, b_igate, lru_lambda):
    S = xc.shape[1]
    t = jnp.arange(S)[None, :, None]
    total = jnp.zeros(xc.shape, jnp.float32)
    for d, reverse in ((0, False), (1, True)):
        r = jax.nn.sigmoid(_block_diag(xc, w_rgate[d], b_rgate[d])).astype(jnp.float32)
        i = jax.nn.sigmoid(_block_diag(xc, w_igate[d], b_igate[d]))
        log_a = -LRU_C * r * jax.nn.softplus(-lru_lambda[d].astype(jnp.float32))
        a = jnp.exp(log_a)
        mult = jnp.sqrt(-jnp.expm1(2.0 * log_a))
        first = t == ((S - 1) if reverse else 0)
        mult = jnp.where(first, 1.0, mult)
        bx = mult * (i * xc).astype(jnp.float32)
        _, h = lax.associative_scan(_lin_combine, (a, bx), axis=1, reverse=reverse)
        total = total + h
    return total.astype(xc.dtype)


def _column_layout():
    qc = np.arange(GRID_W).reshape(GRID_W // NA_QCOLS, NA_QCOLS)
    cs = np.clip(qc - NA_COLS // 2, 0, GRID_W - NA_COLS)
    start = np.minimum(cs[:, 0], GRID_W - NA_KCOLS)
    kc = start[:, None] + np.arange(NA_KCOLS)
    mask = (kc[:, None, :] >= cs[:, :, None]) & (kc[:, None, :] < cs[:, :, None] + NA_COLS)
    dc = np.clip(kc[:, None, :] - qc[:, :, None], -(NA_COLS - 1), NA_COLS - 1) + NA_COLS - 1
    return kc, mask, dc


def neighbourhood_attention(q, k, v, rpb):
    B, S, H, Dh = q.shape
    rows = S // GRID_W
    kh = min(NA_ROWS, rows)
    ncb = GRID_W // NA_QCOLS
    kc, col_mask, dc = _column_layout()
    mask = np.broadcast_to(col_mask[:, :, None, :], (ncb, NA_QCOLS, kh, NA_KCOLS)).reshape(
        ncb, NA_QCOLS, kh * NA_KCOLS)
    qg = q.reshape(B, rows, GRID_W, H, Dh)
    kg = k.reshape(B, rows, GRID_W, H, Dh)
    vg = v.reshape(B, rows, GRID_W, H, Dh)
    rpb32 = rpb.astype(jnp.float32)
    scale = HEAD_DIM ** -0.5

    def row_block(r):
        rs = jnp.clip(r - kh // 2, 0, rows - kh)
        k_rows = lax.dynamic_slice_in_dim(kg, rs, kh, axis=1)
        v_rows = lax.dynamic_slice_in_dim(vg, rs, kh, axis=1)
        k_blk = jnp.moveaxis(k_rows[:, :, kc], 2, 1).reshape(B, ncb, kh * NA_KCOLS, H, Dh)
        v_blk = jnp.moveaxis(v_rows[:, :, kc], 2, 1).reshape(B, ncb, kh * NA_KCOLS, H, Dh)
        q_row = lax.dynamic_index_in_dim(qg, r, axis=1, keepdims=False).reshape(B, ncb, NA_QCOLS, H, Dh)
        s = jnp.einsum('bjqhd,bjkhd->bhjqk', q_row, k_blk, preferred_element_type=jnp.float32) * scale
        dr = rs + jnp.arange(kh) - r + NA_ROWS - 1
        bias = rpb32[:, dr][:, :, dc]
        bias = jnp.transpose(bias, (0, 2, 3, 1, 4)).reshape(H, ncb, NA_QCOLS, kh * NA_KCOLS)
        s = jnp.where(mask, s + bias, NEG_INF)
        w = jax.nn.softmax(s, axis=-1).astype(v.dtype)
        o = jnp.einsum('bhjqk,bjkhd->bjqhd', w, v_blk)
        return o.reshape(B, GRID_W, H, Dh)

    out = lax.map(row_block, jnp.arange(rows))
    return jnp.moveaxis(out, 0, 1).reshape(B, S, H * Dh)


def encoder_layer(x, p, g_mix, w_in, rnn_conv_w, rnn_conv_b, w_rgate, b_rgate, w_igate, b_igate,
                  lru_lambda, rpb, w_rnn_out, w_att_out, w_out, g_ffn, w_up, ffn_conv_w, ffn_conv_b,
                  w_down, g_ple, w_ple_gate, w_ple_proj):
    B, S, _ = x.shape
    h = rmsnorm(x, g_mix)
    z = h @ w_in
    idx = [int(c) for c in np.cumsum(SPLITS)[:-1]]
    x_rnn, y_rnn, q, k, v, gate_a, gate_b = jnp.split(z, idx, axis=-1)
    xc = depthwise_conv(x_rnn, rnn_conv_w, rnn_conv_b, RNN_CONV_W // 2, RNN_CONV_W - 1 - RNN_CONV_W // 2)
    a_out = (rglru_bidir(xc, w_rgate, b_rgate, w_igate, b_igate, lru_lambda) * jax.nn.gelu(y_rnn)) @ w_rnn_out
    att = neighbourhood_attention(q.reshape(B, S, N_HEADS, HEAD_DIM), k.reshape(B, S, N_HEADS, HEAD_DIM),
                                  v.reshape(B, S, N_HEADS, HEAD_DIM), rpb)
    b_out = att @ w_att_out
    mixed = jax.nn.sigmoid(gate_a) * a_out + jax.nn.sigmoid(gate_b) * b_out
    x = x + mixed @ w_out
    h = rmsnorm(x, g_ffn)
    u = depthwise_conv(h @ w_up, ffn_conv_w, ffn_conv_b, FFN_CONV_W // 2, FFN_CONV_W // 2)
    u_gate, u_val = jnp.split(u, 2, axis=-1)
    x = x + (jax.nn.gelu(u_gate) * u_val) @ w_down
    pg = jax.nn.sigmoid(rmsnorm(x, g_ple) @ w_ple_gate)
    x = x + pg * (p @ w_ple_proj)
    return x


def encoder_forward(x, p, layer_params, g_final):
    for i in range(DEPTH):
        x = encoder_layer(x, p[i], **{name: w[i] for name, w in layer_params.items()})
    return rmsnorm(x, g_final)


def setup_inputs(seed: int = 0) -> dict:
    key = jax.random.key(seed)
    ks = jax.random.split(key, 32)
    f32 = jnp.float32

    def nrm(k, shape, scale):
        return jax.random.normal(k, shape, f32) * scale

    u = jax.random.uniform(ks[10], (DEPTH, 2, D_RNN), f32, 0.9, 0.999)
    a0 = u ** (1.0 / LRU_C)
    lru_lambda = jnp.log(a0) - jnp.log1p(-a0)
    return {
        'x_prompt': nrm(ks[0], (BATCH, SEQ, D_MODEL), 1.0),
        'x_sample': nrm(ks[1], (DEC_BATCH, DEC_SEQ, D_MODEL), 1.0),
        'p_prompt': nrm(ks[2], (DEPTH, BATCH, SEQ, PLE_DIM), 1.0),
        'p_sample': nrm(ks[3], (DEPTH, DEC_BATCH, DEC_SEQ, PLE_DIM), 1.0),
        'g_mix': 1.0 + nrm(ks[4], (DEPTH, D_MODEL), 0.01),
        'w_in': nrm(ks[5], (DEPTH, D_MODEL, D_IN), D_MODEL ** -0.5),
        'rnn_conv_w': nrm(ks[6], (DEPTH, RNN_CONV_W, D_RNN), RNN_CONV_W ** -0.5),
        'rnn_conv_b': nrm(ks[7], (DEPTH, D_RNN), 0.01),
        'w_rgate': nrm(ks[8], (DEPTH, 2, RNN_BLOCKS, RNN_BLOCK_W, RNN_BLOCK_W), RNN_BLOCK_W ** -0.5),
        'b_rgate': nrm(ks[9], (DEPTH, 2, D_RNN), 0.01),
        'w_igate': nrm(ks[11], (DEPTH, 2, RNN_BLOCKS, RNN_BLOCK_W, RNN_BLOCK_W), RNN_BLOCK_W ** -0.5),
        'b_igate': nrm(ks[12], (DEPTH, 2, D_RNN), 0.01),
        'lru_lambda': lru_lambda,
        'rpb': nrm(ks[13], (DEPTH, N_HEADS, 2 * NA_ROWS - 1, 2 * NA_COLS - 1), 0.02),
        'w_rnn_out': nrm(ks[14], (DEPTH, D_RNN, D_MODEL), D_RNN ** -0.5),
        'w_att_out': nrm(ks[15], (DEPTH, D_ATT, D_MODEL), D_ATT ** -0.5),
        'w_out': nrm(ks[16], (DEPTH, D_MODEL, D_MODEL), D_MODEL ** -0.5),
        'g_ffn': 1.0 + nrm(ks[17], (DEPTH, D_MODEL), 0.01),
        'w_up': nrm(ks[18], (DEPTH, D_MODEL, 2 * D_FF), D_MODEL ** -0.5),
        'ffn_conv_w': nrm(ks[19], (DEPTH, FFN_CONV_W, 2 * D_FF), FFN_CONV_W ** -0.5),
        'ffn_conv_b': nrm(ks[20], (DEPTH, 2 * D_FF), 0.01),
        'w_down': nrm(ks[21], (DEPTH, D_FF, D_MODEL), D_FF ** -0.5),
        'g_ple': 1.0 + nrm(ks[22], (DEPTH, D_MODEL), 0.01),
        'w_ple_gate': nrm(ks[23], (DEPTH, D_MODEL, D_MODEL), D_MODEL ** -0.5),
        'w_ple_proj': nrm(ks[24], (DEPTH, PLE_DIM, D_MODEL), PLE_DIM ** -0.5),
        'g_final': 1.0 + nrm(ks[25], (D_MODEL,), 0.01),
    }


def reference(x_prompt, x_sample, p_prompt, p_sample, g_mix, w_in, rnn_conv_w, rnn_conv_b, w_rgate,
              b_rgate, w_igate, b_igate, lru_lambda, rpb, w_rnn_out, w_att_out, w_out, g_ffn, w_up,
              ffn_conv_w, ffn_conv_b, w_down, g_ple, w_ple_gate, w_ple_proj, g_final):
    layer_params = dict(g_mix=g_mix, w_in=w_in, rnn_conv_w=rnn_conv_w, rnn_conv_b=rnn_conv_b,
                        w_rgate=w_rgate, b_rgate=b_rgate, w_igate=w_igate, b_igate=b_igate,
                        lru_lambda=lru_lambda, rpb=rpb, w_rnn_out=w_rnn_out, w_att_out=w_att_out,
                        w_out=w_out, g_ffn=g_ffn, w_up=w_up, ffn_conv_w=ffn_conv_w,
                        ffn_conv_b=ffn_conv_b, w_down=w_down, g_ple=g_ple, w_ple_gate=w_ple_gate,
                        w_ple_proj=w_ple_proj)
    y_prompt = encoder_forward(x_prompt, p_prompt, layer_params, g_final)
    y_sample = encoder_forward(x_sample, p_sample, layer_params, g_final)
    return (y_prompt, y_sample)
```

```python
import functools

import numpy as np
import jax
import jax.numpy as jnp
from jax import lax
from jax.experimental import pallas as pl
from jax.experimental.pallas import tpu as pltpu

F32 = jnp.float32
BF16 = jnp.bfloat16

D_MODEL = 1024
GRID_W = 64
PLE_DIM = 256
D_RNN = 1024
RNN_BLOCKS = 8
RNN_BLOCK_W = D_RNN // RNN_BLOCKS
RNN_CONV_W = 4
LRU_C = 8.0
N_HEADS = 8
HEAD_DIM = 128
D_ATT = N_HEADS * HEAD_DIM
NA_ROWS = 8
NA_COLS = 16
D_FF = 3072
EPS = 1e-6
NEG_INF = -1e30
D_IN = 2 * D_RNN + 3 * D_ATT + 2 * D_MODEL
Z_XRNN, Z_YRNN, Z_Q, Z_K, Z_V, Z_GA, Z_GB = range(7)

SUBLANES = 8
BF16_ROWS = 16
VMEM_CAP_BYTES = 64 * 1024 * 1024
VMEM_INTERNAL_BYTES = 16 * 1024 * 1024


def _params(block_bytes, semantics):
    limit = min(2 * block_bytes + VMEM_INTERNAL_BYTES, VMEM_CAP_BYTES - 4 * 1024 * 1024)
    return pltpu.CompilerParams(dimension_semantics=semantics, vmem_limit_bytes=int(limit))


def _nbytes(shape, dtype):
    return int(np.prod(shape)) * jnp.dtype(dtype).itemsize


def _rms(x, g):
    return x * lax.rsqrt(jnp.mean(x * x, axis=-1, keepdims=True) + EPS) * g


def _norm_matmul_kernel(x_ref, g_ref, w_ref, o_ref, h_ref):
    @pl.when(pl.program_id(1) == 0)
    def _():
        h_ref[...] = _rms(x_ref[...], g_ref[...]).astype(BF16)

    o_ref[...] = jnp.dot(h_ref[...], w_ref[...], preferred_element_type=F32).astype(o_ref.dtype)


def _norm_matmul(x, g, w, *, tm, tn):
    t, d = x.shape
    n = w.shape[1]
    assert t % tm == 0 and n % tn == 0
    blocks = (_nbytes((tm, d), F32) + _nbytes((d, tn), BF16) + _nbytes((tm, tn), BF16)
              + _nbytes((tm, d), BF16))
    return pl.pallas_call(
        _norm_matmul_kernel,
        grid=(t // tm, n // tn),
        in_specs=[pl.BlockSpec((tm, d), lambda i, j: (i, 0)),
                  pl.BlockSpec((1, d), lambda i, j: (0, 0)),
                  pl.BlockSpec((d, tn), lambda i, j: (0, j))],
        out_specs=pl.BlockSpec((tm, tn), lambda i, j: (i, j)),
        out_shape=jax.ShapeDtypeStruct((t, n), BF16),
        scratch_shapes=[pltpu.VMEM((tm, d), BF16)],
        compiler_params=_params(blocks, ("arbitrary", "arbitrary")),
        name="norm_matmul",
    )(x, g.reshape(1, d), w)


def _attention_bias(rpb):
    cols = np.arange(GRID_W)
    cs = np.clip(cols - NA_COLS // 2, 0, GRID_W - NA_COLS)
    mask = (cols[None, :] >= cs[:, None]) & (cols[None, :] < cs[:, None] + NA_COLS)
    dc = np.clip(cols[None, :] - cols[:, None], -(NA_COLS - 1), NA_COLS - 1) + NA_COLS - 1
    dr = np.arange(NA_ROWS)[None, :] + np.arange(NA_ROWS)[:, None]
    b = rpb.astype(F32)[:, dr]
    b = b[:, :, :, dc]
    b = jnp.where(mask[None, None, None], b, NEG_INF)
    b = jnp.transpose(b, (0, 1, 3, 2, 4))
    return b.reshape(N_HEADS, NA_ROWS, GRID_W, NA_ROWS * GRID_W)


def _attention_kernel(q_ref, k_ref, v_ref, bias_ref, o_ref, *, rows):
    scale = HEAD_DIM ** -0.5
    band = NA_ROWS * GRID_W

    def row_step(r, carry):
        rs = jnp.clip(r - NA_ROWS // 2, 0, rows - NA_ROWS)
        shift = rs - r + NA_ROWS // 2 + 3
        q0 = pl.multiple_of(r * GRID_W, GRID_W)
        k0 = pl.multiple_of(rs * GRID_W, GRID_W)
        q = q_ref[pl.ds(q0, GRID_W), :]
        k = k_ref[pl.ds(k0, band), :]
        v = v_ref[pl.ds(k0, band), :]
        s = lax.dot_general(q, k, (((1,), (1,)), ((), ())), preferred_element_type=F32)
        s = s * scale + bias_ref[shift]
        p = jnp.exp(s - jnp.max(s, axis=-1, keepdims=True))
        den = jnp.sum(p, axis=-1, keepdims=True)
        o = jnp.dot(p.astype(BF16), v, preferred_element_type=F32) / den
        o_ref[pl.ds(q0, GRID_W), :] = o.astype(o_ref.dtype)
        return carry

    lax.fori_loop(0, rows, row_step, 0)


def _attention(z, bias):
    b, s, _ = z.shape
    rows = s // GRID_W
    assert rows >= NA_ROWS
    hb = D_ATT // HEAD_DIM
    blocks = 4 * _nbytes((s, HEAD_DIM), BF16) + _nbytes(bias.shape[1:], F32)

    def head_spec(split):
        return pl.BlockSpec((None, s, HEAD_DIM), lambda h, i: (i, 0, split * hb + h))

    return pl.pallas_call(
        functools.partial(_attention_kernel, rows=rows),
        grid=(N_HEADS, b),
        in_specs=[head_spec(Z_Q), head_spec(Z_K), head_spec(Z_V),
                  pl.BlockSpec((None,) + bias.shape[1:], lambda h, i: (h, 0, 0, 0))],
        out_specs=pl.BlockSpec((None, s, HEAD_DIM), lambda h, i: (i, 0, h)),
        out_shape=jax.ShapeDtypeStruct((b, s, D_ATT), BF16),
        compiler_params=_params(blocks, ("arbitrary", "arbitrary")),
        name="attention",
    )(z, z, z, bias)


def _shift_rows(x, prev_rows, next_rows, k):
    tt = x.shape[0]
    row = lax.broadcasted_iota(jnp.int32, (tt, 1), 0)
    y = pltpu.roll(x, (-k) % tt, 0)
    if k < 0:
        for i in range(-k):
            y = jnp.where(row == i, prev_rows[i:i + 1], y)
    else:
        for i in range(k):
            y = jnp.where(row == tt - k + i, next_rows[i:i + 1], y)
    return y


def _group_scan(a, b, reverse):
    sub = lax.broadcasted_iota(jnp.int32, (a.shape[0], 1), 0) % SUBLANES
    d = 1
    while d < SUBLANES:
        if reverse:
            keep = sub < SUBLANES - d
            a_n = pltpu.roll(a, a.shape[0] - d, 0)
            b_n = pltpu.roll(b, a.shape[0] - d, 0)
        else:
            keep = sub >= d
            a_n = pltpu.roll(a, d, 0)
            b_n = pltpu.roll(b, d, 0)
        b = b + a * jnp.where(keep, b_n, 0.0)
        a = a * jnp.where(keep, a_n, 1.0)
        d *= 2
    return a, b


def _rglru_tile(ti, n_t, x_ref, xp_ref, xn_ref, cw_ref, cb_ref, wg_ref, bg_ref, lam_ref,
                carry_ref, a_ref, h_ref, *, reverse, seq_len):
    tt = x_ref.shape[0]
    x = x_ref[...].astype(F32)
    xp = jnp.where(ti > 0, xp_ref[...].astype(F32), 0.0)[BF16_ROWS - 2:]
    xn = jnp.where(ti < n_t - 1, xn_ref[...].astype(F32), 0.0)[:1]
    xc = (cw_ref[0:1] * _shift_rows(x, xp, xn, -2) + cw_ref[1:2] * _shift_rows(x, xp[1:], xn, -1)
          + cw_ref[2:3] * x + cw_ref[3:4] * _shift_rows(x, xp, xn, 1) + cb_ref[...])
    xcb = xc.astype(BF16)
    sp = jax.nn.softplus(-lam_ref[...])
    t_glob = ti * tt + lax.broadcasted_iota(jnp.int32, (tt, 1), 0)
    first = t_glob == (seq_len - 1 if reverse else 0)
    for n in range(RNN_BLOCKS):
        sl = slice(n * RNN_BLOCK_W, (n + 1) * RNN_BLOCK_W)
        g = jnp.dot(xcb[:, sl], wg_ref[n], preferred_element_type=F32)
        r = jax.nn.sigmoid(g[:, :RNN_BLOCK_W] + bg_ref[0:1, sl])
        i = jax.nn.sigmoid(g[:, RNN_BLOCK_W:] + bg_ref[1:2, sl])
        neg_log_a = LRU_C * r * sp[:, sl]
        a = jnp.exp(-neg_log_a)
        th = jnp.tanh(neg_log_a)
        mult = jnp.where(first, 1.0, jnp.sqrt(2.0 * th / (1.0 + th)))
        a_grp, b_grp = _group_scan(a, mult * (i * xc[:, sl]), reverse)
        a_ref[:, sl] = a_grp
        h_ref[:, sl] = b_grp

    groups = tt // SUBLANES

    def step(gidx, hb):
        g0 = pl.multiple_of((groups - 1 - gidx if reverse else gidx) * SUBLANES, SUBLANES)
        h = a_ref[pl.ds(g0, SUBLANES), :] * hb + h_ref[pl.ds(g0, SUBLANES), :]
        h_ref[pl.ds(g0, SUBLANES), :] = h
        edge = h[0:1] if reverse else h[SUBLANES - 1:]
        return jnp.broadcast_to(edge, h.shape)

    carry_ref[...] = lax.fori_loop(0, groups, step, carry_ref[...], unroll=4)


def _rglru_bwd_kernel(x_ref, xp_ref, xn_ref, cw_ref, cb_ref, wg_ref, bg_ref, lam_ref, o_ref,
                      carry_ref, a_ref, h_ref, *, seq_len):
    j = pl.program_id(1)
    n_t = pl.num_programs(1)

    @pl.when(j == 0)
    def _():
        carry_ref[...] = jnp.zeros_like(carry_ref)

    _rglru_tile(n_t - 1 - j, n_t, x_ref, xp_ref, xn_ref, cw_ref, cb_ref, wg_ref, bg_ref, lam_ref,
                carry_ref, a_ref, h_ref, reverse=True, seq_len=seq_len)
    o_ref[...] = h_ref[...].astype(o_ref.dtype)


def _rglru_fwd_merge_kernel(x_ref, xp_ref, xn_ref, y_ref, ga_ref, gb_ref, att_ref, hbwd_ref, res_ref,
                            cw_ref, cb_ref, wg_ref, bg_ref, lam_ref, wa_ref, wb_ref, wo_ref, o_ref,
                            carry_ref, a_ref, h_ref, *, seq_len):
    j = pl.program_id(1)
    n_t = pl.num_programs(1)

    @pl.when(j == 0)
    def _():
        carry_ref[...] = jnp.zeros_like(carry_ref)

    _rglru_tile(j, n_t, x_ref, xp_ref, xn_ref, cw_ref, cb_ref, wg_ref, bg_ref, lam_ref,
                carry_ref, a_ref, h_ref, reverse=False, seq_len=seq_len)
    h_tot = h_ref[...] + hbwd_ref[...].astype(F32)
    m = (h_tot * jax.nn.gelu(y_ref[...].astype(F32))).astype(BF16)
    a_out = jnp.dot(m, wa_ref[...], preferred_element_type=F32)
    b_out = jnp.dot(att_ref[...], wb_ref[...], preferred_element_type=F32)
    mixed = (jax.nn.sigmoid(ga_ref[...].astype(F32)) * a_out
             + jax.nn.sigmoid(gb_ref[...].astype(F32)) * b_out)
    o_ref[...] = res_ref[...] + jnp.dot(mixed.astype(BF16), wo_ref[...], preferred_element_type=F32)


def _tile_specs(tt, width, col, tile_of_step, n_t):
    per = tt // BF16_ROWS
    last = n_t * per - 1
    cur = pl.BlockSpec((None, tt, width), lambda b, j: (b, tile_of_step(j), col))
    prev = pl.BlockSpec((None, BF16_ROWS, width),
                        lambda b, j: (b, jnp.maximum(tile_of_step(j) * per - 1, 0), col))
    nxt = pl.BlockSpec((None, BF16_ROWS, width),
                       lambda b, j: (b, jnp.minimum((tile_of_step(j) + 1) * per, last), col))
    return cur, prev, nxt


def _full_spec(shape):
    return pl.BlockSpec(shape, lambda b, j: (0,) * len(shape))


def _rglru_bwd(z, cw, cb, wg, bg, lam, *, tt):
    b, s, _ = z.shape
    n_t = s // tt
    rev = lambda j: n_t - 1 - j
    cur, prev, nxt = _tile_specs(tt, D_RNN, Z_XRNN, rev, n_t)
    blocks = (_nbytes((tt, D_RNN), BF16) * 2 + _nbytes((BF16_ROWS, D_RNN), BF16) * 2
              + _nbytes(wg.shape, BF16) + 3 * _nbytes((tt, D_RNN), F32))
    return pl.pallas_call(
        functools.partial(_rglru_bwd_kernel, seq_len=s),
        grid=(b, n_t),
        in_specs=[cur, prev, nxt, _full_spec(cw.shape), _full_spec(cb.shape), _full_spec(wg.shape),
                  _full_spec(bg.shape), _full_spec(lam.shape)],
        out_specs=pl.BlockSpec((None, tt, D_RNN), lambda i, j: (i, rev(j), 0)),
        out_shape=jax.ShapeDtypeStruct((b, s, D_RNN), BF16),
        scratch_shapes=[pltpu.VMEM((SUBLANES, D_RNN), F32), pltpu.VMEM((tt, D_RNN), F32),
                        pltpu.VMEM((tt, D_RNN), F32)],
        compiler_params=_params(blocks, ("arbitrary", "arbitrary")),
        name="rglru_bwd",
    )(z, z, z, cw, cb, wg, bg, lam)


def _rglru_fwd_merge(z, att, hbwd, x, cw, cb, wg, bg, lam, wa, wb, wo, *, tt):
    b, s, _ = z.shape
    n_t = s // tt
    fwd = lambda j: j
    cur, prev, nxt = _tile_specs(tt, D_RNN, Z_XRNN, fwd, n_t)

    def tile(col):
        return pl.BlockSpec((None, tt, D_MODEL), lambda i, j: (i, j, col))

    blocks = (_nbytes((tt, D_MODEL), BF16) * 6 + _nbytes((BF16_ROWS, D_RNN), BF16) * 2
              + _nbytes((tt, D_MODEL), F32) * 2 + _nbytes(wg.shape, BF16)
              + 3 * _nbytes((D_MODEL, D_MODEL), BF16) + 3 * _nbytes((tt, D_RNN), F32))
    return pl.pallas_call(
        functools.partial(_rglru_fwd_merge_kernel, seq_len=s),
        grid=(b, n_t),
        in_specs=[cur, prev, nxt, tile(Z_YRNN), tile(Z_GA), tile(Z_GB), tile(0), tile(0), tile(0),
                  _full_spec(cw.shape), _full_spec(cb.shape), _full_spec(wg.shape), _full_spec(bg.shape),
                  _full_spec(lam.shape), _full_spec(wa.shape), _full_spec(wb.shape), _full_spec(wo.shape)],
        out_specs=tile(0),
        out_shape=jax.ShapeDtypeStruct((b, s, D_MODEL), F32),
        scratch_shapes=[pltpu.VMEM((SUBLANES, D_RNN), F32), pltpu.VMEM((tt, D_RNN), F32),
                        pltpu.VMEM((tt, D_RNN), F32)],
        compiler_params=_params(blocks, ("arbitrary", "arbitrary")),
        name="rglru_fwd_merge",
    )(z, z, z, z, z, z, att, hbwd, x, cw, cb, wg, bg, lam, wa, wb, wo)


FFN_CHUNK = 512


def _ffn_kernel(u_ref, up_ref, un_ref, x_ref, p_ref, cw_ref, cb_ref, wd_ref, gp_ref, wpg_ref, wpp_ref,
                gf_ref, o_ref):
    ti = pl.program_id(1)
    n_t = pl.num_programs(1)

    def conv(c0):
        cs = slice(c0, c0 + FFN_CHUNK)
        u = u_ref[:, cs].astype(F32)
        prev = jnp.where(ti > 0, up_ref[:, cs].astype(F32), 0.0)[BF16_ROWS - 1:]
        nxt = jnp.where(ti < n_t - 1, un_ref[:, cs].astype(F32), 0.0)[:1]
        return (cw_ref[0:1, cs] * _shift_rows(u, prev, nxt, -1) + cw_ref[1:2, cs] * u
                + cw_ref[2:3, cs] * _shift_rows(u, prev, nxt, 1) + cb_ref[:, cs])

    acc = x_ref[...]
    for c in range(D_FF // FFN_CHUNK):
        g = (jax.nn.gelu(conv(c * FFN_CHUNK)) * conv(D_FF + c * FFN_CHUNK)).astype(BF16)
        acc = acc + jnp.dot(g, wd_ref[c * FFN_CHUNK:(c + 1) * FFN_CHUNK, :], preferred_element_type=F32)
    hp = _rms(acc, gp_ref[...]).astype(BF16)
    pg = jax.nn.sigmoid(jnp.dot(hp, wpg_ref[...], preferred_element_type=F32))
    pp = jnp.dot(p_ref[...].astype(BF16), wpp_ref[...], preferred_element_type=F32)
    o_ref[...] = _rms(acc + pg * pp, gf_ref[...])


def _ffn(u, x1, p, cw, cb, wd, gp, wpg, wpp, gf, *, tt):
    b, s, _ = u.shape
    n_t = s // tt
    cur, prev, nxt = _tile_specs(tt, 2 * D_FF, 0, lambda j: j, n_t)

    def tile(width):
        return pl.BlockSpec((None, tt, width), lambda i, j: (i, j, 0))

    blocks = (_nbytes((tt, 2 * D_FF), BF16) + 2 * _nbytes((BF16_ROWS, 2 * D_FF), BF16)
              + 2 * _nbytes((tt, D_MODEL), F32) + _nbytes((tt, PLE_DIM), F32) + _nbytes(wd.shape, BF16)
              + _nbytes(wpg.shape, BF16) + _nbytes(wpp.shape, BF16))
    return pl.pallas_call(
        _ffn_kernel,
        grid=(b, n_t),
        in_specs=[cur, prev, nxt, tile(D_MODEL), tile(PLE_DIM), _full_spec(cw.shape), _full_spec(cb.shape),
                  _full_spec(wd.shape), _full_spec(gp.shape), _full_spec(wpg.shape), _full_spec(wpp.shape),
                  _full_spec(gf.shape)],
        out_specs=tile(D_MODEL),
        out_shape=jax.ShapeDtypeStruct((b, s, D_MODEL), F32),
        compiler_params=_params(blocks, ("arbitrary", "arbitrary")),
        name="ffn_out",
    )(u, u, u, x1, p, cw, cb, wd, gp, wpg, wpp, gf)


PROJ_TM = 1024
PROJ_TN = 1024
RNN_TT = 256
FFN_TT = 256


def _layer(x, p, w):
    b, s, d = x.shape
    t = b * s
    z = _norm_matmul(x.reshape(t, d), w["g_mix"], w["w_in"], tm=PROJ_TM, tn=PROJ_TN).reshape(b, s, D_IN)
    att = _attention(z, w["att_bias"])
    rnn = (w["rnn_conv_w"], w["rnn_conv_b"])
    hbwd = _rglru_bwd(z, *rnn, w["wg"][1], w["bg"][1], w["lam"][1], tt=RNN_TT)
    x1 = _rglru_fwd_merge(z, att, hbwd, x, *rnn, w["wg"][0], w["bg"][0], w["lam"][0],
                          w["w_rnn_out"], w["w_att_out"], w["w_out"], tt=RNN_TT)
    u = _norm_matmul(x1.reshape(t, d), w["g_ffn"], w["w_up"], tm=PROJ_TM, tn=PROJ_TN).reshape(b, s, 2 * D_FF)
    return _ffn(u, x1, p, w["ffn_conv_w"], w["ffn_conv_b"], w["w_down"], w["g_ple"], w["w_ple_gate"],
                w["w_ple_proj"], w["g_final"], tt=FFN_TT)


def kernel(x_prompt, x_sample, p_prompt, p_sample, g_mix, w_in, rnn_conv_w, rnn_conv_b, w_rgate, b_rgate,
           w_igate, b_igate, lru_lambda, rpb, w_rnn_out, w_att_out, w_out, g_ffn, w_up, ffn_conv_w,
           ffn_conv_b, w_down, g_ple, w_ple_gate, w_ple_proj, g_final):
    assert g_mix.shape[0] == 1, "single-layer trunk"
    row = lambda v: v.reshape(1, -1)
    w = {
        "g_mix": g_mix[0], "w_in": w_in[0].astype(BF16),
        "rnn_conv_w": rnn_conv_w[0], "rnn_conv_b": row(rnn_conv_b[0]),
        "wg": jnp.concatenate([w_rgate[0], w_igate[0]], axis=-1).astype(BF16),
        "bg": jnp.stack([b_rgate[0], b_igate[0]], axis=1),
        "lam": lru_lambda[0][:, None, :],
        "att_bias": _attention_bias(rpb[0]),
        "w_rnn_out": w_rnn_out[0].astype(BF16), "w_att_out": w_att_out[0].astype(BF16),
        "w_out": w_out[0].astype(BF16),
        "g_ffn": g_ffn[0], "w_up": w_up[0].astype(BF16),
        "ffn_conv_w": ffn_conv_w[0], "ffn_conv_b": row(ffn_conv_b[0]),
        "w_down": w_down[0].astype(BF16), "g_ple": row(g_ple[0]),
        "w_ple_gate": w_ple_gate[0].astype(BF16), "w_ple_proj": w_ple_proj[0].astype(BF16),
        "g_final": row(g_final),
    }
    return _layer(x_prompt, p_prompt[0], w), _layer(x_sample, p_sample[0], w)
```

```python
import functools

import numpy as np
import jax
import jax.numpy as jnp
from jax import lax
from jax.experimental import pallas as pl
from jax.experimental.pallas import tpu as pltpu

F32 = jnp.float32
BF16 = jnp.bfloat16

D_MODEL = 1024
GRID_W = 64
PLE_DIM = 256
D_RNN = 1024
RNN_BLOCKS = 8
RNN_BLOCK_W = D_RNN // RNN_BLOCKS
RNN_CONV_W = 4
LRU_C = 8.0
N_HEADS = 8
HEAD_DIM = 128
D_ATT = N_HEADS * HEAD_DIM
NA_ROWS = 8
NA_COLS = 16
D_FF = 3072
EPS = 1e-6
NEG_INF = -1e30
LOG2E = 1.4426950408889634
D_IN = 2 * D_RNN + 3 * D_ATT + 2 * D_MODEL
Z_XRNN, Z_YRNN, Z_Q, Z_K, Z_V, Z_GA, Z_GB = range(7)

SUBLANES = 8
BF16_ROWS = 16
VMEM_CAP_BYTES = 64 * 1024 * 1024
VMEM_INTERNAL_BYTES = 16 * 1024 * 1024


def _params(block_bytes, semantics):
    limit = min(2 * block_bytes + VMEM_INTERNAL_BYTES, VMEM_CAP_BYTES - 4 * 1024 * 1024)
    return pltpu.CompilerParams(dimension_semantics=semantics, vmem_limit_bytes=int(limit))


def _nbytes(shape, dtype):
    return int(np.prod(shape)) * jnp.dtype(dtype).itemsize


def _rms(x, g):
    return x * lax.rsqrt(jnp.mean(x * x, axis=-1, keepdims=True) + EPS) * g


def _norm_matmul_kernel(x_ref, g_ref, w_ref, o_ref, h_ref):
    @pl.when(pl.program_id(1) == 0)
    def _():
        h_ref[...] = _rms(x_ref[...], g_ref[...]).astype(BF16)

    o_ref[...] = jnp.dot(h_ref[...], w_ref[...], preferred_element_type=F32).astype(o_ref.dtype)


def _norm_matmul(x, g, w, *, tm, tn):
    t, d = x.shape
    n = w.shape[1]
    assert t % tm == 0 and n % tn == 0
    blocks = (_nbytes((tm, d), F32) + _nbytes((d, tn), BF16) + _nbytes((tm, tn), BF16)
              + _nbytes((tm, d), BF16))
    return pl.pallas_call(
        _norm_matmul_kernel,
        grid=(t // tm, n // tn),
        in_specs=[pl.BlockSpec((tm, d), lambda i, j: (i, 0)),
                  pl.BlockSpec((1, d), lambda i, j: (0, 0)),
                  pl.BlockSpec((d, tn), lambda i, j: (0, j))],
        out_specs=pl.BlockSpec((tm, tn), lambda i, j: (i, j)),
        out_shape=jax.ShapeDtypeStruct((t, n), BF16),
        scratch_shapes=[pltpu.VMEM((tm, d), BF16)],
        compiler_params=_params(blocks, ("arbitrary", "arbitrary")),
        name="norm_matmul",
    )(x, g.reshape(1, d), w)


ATT_ROWS = 4
ATT_WIN = ATT_ROWS + NA_ROWS
ATT_HEADS = 2


def _attention_bias(rpb, rows):
    assert rows % ATT_ROWS == 0 and rows >= ATT_WIN + ATT_ROWS
    cols = np.arange(GRID_W)
    cs = np.clip(cols - NA_COLS // 2, 0, GRID_W - NA_COLS)
    col_mask = (cols[None, :] >= cs[:, None]) & (cols[None, :] < cs[:, None] + NA_COLS)
    dc = np.clip(cols[None, :] - cols[:, None], -(NA_COLS - 1), NA_COLS - 1) + NA_COLS - 1
    tiles = jnp.where(col_mask, rpb.astype(F32)[:, :, dc] * LOG2E, NEG_INF)
    masked = jnp.full((N_HEADS, GRID_W, GRID_W), NEG_INF, F32)
    variants = []
    for r0 in (0, ATT_ROWS, rows - ATT_ROWS):
        ws = int(np.clip(r0 - NA_ROWS // 2, 0, rows - ATT_WIN))
        q_rows = []
        for r in range(r0, r0 + ATT_ROWS):
            rs = int(np.clip(r - NA_ROWS // 2, 0, rows - NA_ROWS))
            q_rows.append(jnp.concatenate(
                [tiles[:, kr - r + NA_ROWS - 1] if rs <= kr < rs + NA_ROWS else masked
                 for kr in range(ws, ws + ATT_WIN)], axis=-1))
        variants.append(jnp.concatenate(q_rows, axis=1))
    return jnp.stack(variants, axis=1)


def _attention_kernel(q_ref, k_ref, v_ref, bias_ref, o_ref, *, rows):
    scale = HEAD_DIM ** -0.5
    n_blk = rows // ATT_ROWS
    qn = ATT_ROWS * GRID_W
    kn = ATT_WIN * GRID_W

    def block(m, carry):
        ws = jnp.clip(m * ATT_ROWS - NA_ROWS // 2, 0, rows - ATT_WIN)
        variant = jnp.where(m == 0, 0, jnp.where(m == n_blk - 1, 2, 1))
        q0 = pl.multiple_of(m * qn, qn)
        k0 = pl.multiple_of(ws * GRID_W, qn)
        for h in range(ATT_HEADS):
            hs = slice(h * HEAD_DIM, (h + 1) * HEAD_DIM)
            q = q_ref[pl.ds(q0, qn), hs]
            k = k_ref[pl.ds(k0, kn), hs]
            v = v_ref[pl.ds(k0, kn), hs]
            s = lax.dot_general(q, k, (((1,), (1,)), ((), ())), preferred_element_type=F32)
            s = s * (scale * LOG2E) + bias_ref[h, variant]
            p = jnp.exp2(s - jnp.max(s, axis=-1, keepdims=True))
            den = jnp.sum(p, axis=-1, keepdims=True)
            o = jnp.dot(p.astype(BF16), v, preferred_element_type=F32) / den
            o_ref[pl.ds(q0, qn), hs] = o.astype(o_ref.dtype)
        return carry

    lax.fori_loop(0, n_blk, block, 0, unroll=2)


def _attention(z, bias):
    b, s, _ = z.shape
    rows = s // GRID_W
    width = ATT_HEADS * HEAD_DIM
    per_split = D_ATT // width
    blocks = 4 * _nbytes((s, width), BF16) + _nbytes((ATT_HEADS,) + bias.shape[1:], F32)

    def head_spec(split):
        return pl.BlockSpec((None, s, width), lambda h, i: (i, 0, split * per_split + h))

    return pl.pallas_call(
        functools.partial(_attention_kernel, rows=rows),
        grid=(N_HEADS // ATT_HEADS, b),
        in_specs=[head_spec(Z_Q), head_spec(Z_K), head_spec(Z_V),
                  pl.BlockSpec((ATT_HEADS,) + bias.shape[1:], lambda h, i: (h, 0, 0, 0))],
        out_specs=pl.BlockSpec((None, s, width), lambda h, i: (i, 0, h)),
        out_shape=jax.ShapeDtypeStruct((b, s, D_ATT), BF16),
        compiler_params=_params(blocks, ("arbitrary", "arbitrary")),
        name="attention",
    )(z, z, z, bias)


def _with_halo(x, prev_ref, next_ref, ti, n_t):
    prev = jnp.where(ti > 0, prev_ref[...].astype(F32), 0.0)[BF16_ROWS - SUBLANES:]
    nxt = jnp.where(ti < n_t - 1, next_ref[...].astype(F32), 0.0)[:SUBLANES]
    return jnp.concatenate([prev, x, nxt], axis=0)


def _shift_rows(x_ext, k):
    n = x_ext.shape[0]
    return pltpu.roll(x_ext, (-k) % n, 0)[SUBLANES:n - SUBLANES]


def _group_scan(a, b, reverse):
    sub = lax.broadcasted_iota(jnp.int32, (a.shape[0], 1), 0) % SUBLANES
    d = 1
    while d < SUBLANES:
        if reverse:
            keep = sub < SUBLANES - d
            a_n = pltpu.roll(a, a.shape[0] - d, 0)
            b_n = pltpu.roll(b, a.shape[0] - d, 0)
        else:
            keep = sub >= d
            a_n = pltpu.roll(a, d, 0)
            b_n = pltpu.roll(b, d, 0)
        b = b + a * jnp.where(keep, b_n, 0.0)
        a = a * jnp.where(keep, a_n, 1.0)
        d *= 2
    return a, b


def _rglru_tile(ti, n_t, x_ref, xp_ref, xn_ref, cw_ref, cb_ref, wg_ref, bg_ref, lam_ref,
                carry_ref, a_ref, h_ref, *, reverse, seq_len):
    tt = x_ref.shape[0]
    x = x_ref[...].astype(F32)
    x_ext = _with_halo(x, xp_ref, xn_ref, ti, n_t)
    xc = (cw_ref[0:1] * _shift_rows(x_ext, -2) + cw_ref[1:2] * _shift_rows(x_ext, -1)
          + cw_ref[2:3] * x + cw_ref[3:4] * _shift_rows(x_ext, 1) + cb_ref[...])
    xcb = xc.astype(BF16)
    sp = jax.nn.softplus(-lam_ref[...])
    t_glob = ti * tt + lax.broadcasted_iota(jnp.int32, (tt, 1), 0)
    first = t_glob == (seq_len - 1 if reverse else 0)
    for n in range(RNN_BLOCKS):
        sl = slice(n * RNN_BLOCK_W, (n + 1) * RNN_BLOCK_W)
        g = jnp.dot(xcb[:, sl], wg_ref[n], preferred_element_type=F32)
        r = jax.nn.sigmoid(g[:, :RNN_BLOCK_W] + bg_ref[0:1, sl])
        i = jax.nn.sigmoid(g[:, RNN_BLOCK_W:] + bg_ref[1:2, sl])
        neg_log_a = LRU_C * r * sp[:, sl]
        a = jnp.exp(-neg_log_a)
        th = jnp.tanh(neg_log_a)
        mult = jnp.where(first, 1.0, jnp.sqrt(2.0 * th / (1.0 + th)))
        a_grp, b_grp = _group_scan(a, mult * (i * xc[:, sl]), reverse)
        a_ref[:, sl] = a_grp
        h_ref[:, sl] = b_grp

    groups = tt // SUBLANES

    def step(gidx, hb):
        g0 = pl.multiple_of((groups - 1 - gidx if reverse else gidx) * SUBLANES, SUBLANES)
        h = a_ref[pl.ds(g0, SUBLANES), :] * hb + h_ref[pl.ds(g0, SUBLANES), :]
        h_ref[pl.ds(g0, SUBLANES), :] = h
        edge = h[0:1] if reverse else h[SUBLANES - 1:]
        return jnp.broadcast_to(edge, h.shape)

    carry_ref[...] = lax.fori_loop(0, groups, step, carry_ref[...], unroll=4)


def _rglru_bwd_kernel(x_ref, xp_ref, xn_ref, cw_ref, cb_ref, wg_ref, bg_ref, lam_ref, o_ref,
                      carry_ref, a_ref, h_ref, *, seq_len):
    j = pl.program_id(1)
    n_t = pl.num_programs(1)

    @pl.when(j == 0)
    def _():
        carry_ref[...] = jnp.zeros_like(carry_ref)

    _rglru_tile(n_t - 1 - j, n_t, x_ref, xp_ref, xn_ref, cw_ref, cb_ref, wg_ref, bg_ref, lam_ref,
                carry_ref, a_ref, h_ref, reverse=True, seq_len=seq_len)
    o_ref[...] = h_ref[...].astype(o_ref.dtype)


def _rglru_fwd_merge_kernel(x_ref, xp_ref, xn_ref, y_ref, ga_ref, gb_ref, att_ref, hbwd_ref, res_ref,
                            cw_ref, cb_ref, wg_ref, bg_ref, lam_ref, wa_ref, wb_ref, wo_ref, o_ref,
                            carry_ref, a_ref, h_ref, *, seq_len):
    j = pl.program_id(1)
    n_t = pl.num_programs(1)

    @pl.when(j == 0)
    def _():
        carry_ref[...] = jnp.zeros_like(carry_ref)

    _rglru_tile(j, n_t, x_ref, xp_ref, xn_ref, cw_ref, cb_ref, wg_ref, bg_ref, lam_ref,
                carry_ref, a_ref, h_ref, reverse=False, seq_len=seq_len)
    h_tot = h_ref[...] + hbwd_ref[...].astype(F32)
    m = (h_tot * jax.nn.gelu(y_ref[...].astype(F32))).astype(BF16)
    a_out = jnp.dot(m, wa_ref[...], preferred_element_type=F32)
    b_out = jnp.dot(att_ref[...], wb_ref[...], preferred_element_type=F32)
    mixed = (jax.nn.sigmoid(ga_ref[...].astype(F32)) * a_out
             + jax.nn.sigmoid(gb_ref[...].astype(F32)) * b_out)
    o_ref[...] = res_ref[...] + jnp.dot(mixed.astype(BF16), wo_ref[...], preferred_element_type=F32)


def _tile_specs(tt, width, col, tile_of_step, n_t):
    per = tt // BF16_ROWS
    last = n_t * per - 1
    cur = pl.BlockSpec((None, tt, width), lambda b, j: (b, tile_of_step(j), col))
    prev = pl.BlockSpec((None, BF16_ROWS, width),
                        lambda b, j: (b, jnp.maximum(tile_of_step(j) * per - 1, 0), col))
    nxt = pl.BlockSpec((None, BF16_ROWS, width),
                       lambda b, j: (b, jnp.minimum((tile_of_step(j) + 1) * per, last), col))
    return cur, prev, nxt


def _full_spec(shape):
    return pl.BlockSpec(shape, lambda b, j: (0,) * len(shape))


def _rglru_bwd(z, cw, cb, wg, bg, lam, *, tt):
    b, s, _ = z.shape
    n_t = s // tt
    rev = lambda j: n_t - 1 - j
    cur, prev, nxt = _tile_specs(tt, D_RNN, Z_XRNN, rev, n_t)
    blocks = (_nbytes((tt, D_RNN), BF16) * 2 + _nbytes((BF16_ROWS, D_RNN), BF16) * 2
              + _nbytes(wg.shape, BF16) + 3 * _nbytes((tt, D_RNN), F32))
    return pl.pallas_call(
        functools.partial(_rglru_bwd_kernel, seq_len=s),
        grid=(b, n_t),
        in_specs=[cur, prev, nxt, _full_spec(cw.shape), _full_spec(cb.shape), _full_spec(wg.shape),
                  _full_spec(bg.shape), _full_spec(lam.shape)],
        out_specs=pl.BlockSpec((None, tt, D_RNN), lambda i, j: (i, rev(j), 0)),
        out_shape=jax.ShapeDtypeStruct((b, s, D_RNN), BF16),
        scratch_shapes=[pltpu.VMEM((SUBLANES, D_RNN), F32), pltpu.VMEM((tt, D_RNN), F32),
                        pltpu.VMEM((tt, D_RNN), F32)],
        compiler_params=_params(blocks, ("arbitrary", "arbitrary")),
        name="rglru_bwd",
    )(z, z, z, cw, cb, wg, bg, lam)


def _rglru_fwd_merge(z, att, hbwd, x, cw, cb, wg, bg, lam, wa, wb, wo, *, tt):
    b, s, _ = z.shape
    n_t = s // tt
    fwd = lambda j: j
    cur, prev, nxt = _tile_specs(tt, D_RNN, Z_XRNN, fwd, n_t)

    def tile(col):
        return pl.BlockSpec((None, tt, D_MODEL), lambda i, j: (i, j, col))

    blocks = (_nbytes((tt, D_MODEL), BF16) * 6 + _nbytes((BF16_ROWS, D_RNN), BF16) * 2
              + _nbytes((tt, D_MODEL), F32) * 2 + _nbytes(wg.shape, BF16)
              + 3 * _nbytes((D_MODEL, D_MODEL), BF16) + 3 * _nbytes((tt, D_RNN), F32))
    return pl.pallas_call(
        functools.partial(_rglru_fwd_merge_kernel, seq_len=s),
        grid=(b, n_t),
        in_specs=[cur, prev, nxt, tile(Z_YRNN), tile(Z_GA), tile(Z_GB), tile(0), tile(0), tile(0),
                  _full_spec(cw.shape), _full_spec(cb.shape), _full_spec(wg.shape), _full_spec(bg.shape),
                  _full_spec(lam.shape), _full_spec(wa.shape), _full_spec(wb.shape), _full_spec(wo.shape)],
        out_specs=tile(0),
        out_shape=jax.ShapeDtypeStruct((b, s, D_MODEL), F32),
        scratch_shapes=[pltpu.VMEM((SUBLANES, D_RNN), F32), pltpu.VMEM((tt, D_RNN), F32),
                        pltpu.VMEM((tt, D_RNN), F32)],
        compiler_params=_params(blocks, ("arbitrary", "arbitrary")),
        name="rglru_fwd_merge",
    )(z, z, z, z, z, z, att, hbwd, x, cw, cb, wg, bg, lam, wa, wb, wo)


FFN_CHUNK = 512


def _ffn_kernel(u_ref, up_ref, un_ref, x_ref, p_ref, cw_ref, cb_ref, wd_ref, gp_ref, wpg_ref, wpp_ref,
                gf_ref, o_ref):
    ti = pl.program_id(1)
    n_t = pl.num_programs(1)

    def conv(c0):
        cs = slice(c0, c0 + FFN_CHUNK)
        u = u_ref[:, cs].astype(F32)
        u_ext = _with_halo(u, up_ref.at[:, cs], un_ref.at[:, cs], ti, n_t)
        return (cw_ref[0:1, cs] * _shift_rows(u_ext, -1) + cw_ref[1:2, cs] * u
                + cw_ref[2:3, cs] * _shift_rows(u_ext, 1) + cb_ref[:, cs])

    acc = x_ref[...]
    for c in range(D_FF // FFN_CHUNK):
        g = (jax.nn.gelu(conv(c * FFN_CHUNK)) * conv(D_FF + c * FFN_CHUNK)).astype(BF16)
        acc = acc + jnp.dot(g, wd_ref[c * FFN_CHUNK:(c + 1) * FFN_CHUNK, :], preferred_element_type=F32)
    hp = _rms(acc, gp_ref[...]).astype(BF16)
    pg = jax.nn.sigmoid(jnp.dot(hp, wpg_ref[...], preferred_element_type=F32))
    pp = jnp.dot(p_ref[...].astype(BF16), wpp_ref[...], preferred_element_type=F32)
    o_ref[...] = _rms(acc + pg * pp, gf_ref[...])


def _ffn(u, x1, p, cw, cb, wd, gp, wpg, wpp, gf, *, tt):
    b, s, _ = u.shape
    n_t = s // tt
    cur, prev, nxt = _tile_specs(tt, 2 * D_FF, 0, lambda j: j, n_t)

    def tile(width):
        return pl.BlockSpec((None, tt, width), lambda i, j: (i, j, 0))

    blocks = (_nbytes((tt, 2 * D_FF), BF16) + 2 * _nbytes((BF16_ROWS, 2 * D_FF), BF16)
              + 2 * _nbytes((tt, D_MODEL), F32) + _nbytes((tt, PLE_DIM), F32) + _nbytes(wd.shape, BF16)
              + _nbytes(wpg.shape, BF16) + _nbytes(wpp.shape, BF16))
    return pl.pallas_call(
        _ffn_kernel,
        grid=(b, n_t),
        in_specs=[cur, prev, nxt, tile(D_MODEL), tile(PLE_DIM), _full_spec(cw.shape), _full_spec(cb.shape),
                  _full_spec(wd.shape), _full_spec(gp.shape), _full_spec(wpg.shape), _full_spec(wpp.shape),
                  _full_spec(gf.shape)],
        out_specs=tile(D_MODEL),
        out_shape=jax.ShapeDtypeStruct((b, s, D_MODEL), F32),
        compiler_params=_params(blocks, ("arbitrary", "arbitrary")),
        name="ffn_out",
    )(u, u, u, x1, p, cw, cb, wd, gp, wpg, wpp, gf)


PROJ_TM = 1024
PROJ_TN = 1024
RNN_TT = 256
FFN_TT = 256


def _layer(x, p, w):
    b, s, d = x.shape
    t = b * s
    z = _norm_matmul(x.reshape(t, d), w["g_mix"], w["w_in"], tm=PROJ_TM, tn=PROJ_TN).reshape(b, s, D_IN)
    att = _attention(z, _attention_bias(w["rpb"], s // GRID_W))
    rnn = (w["rnn_conv_w"], w["rnn_conv_b"])
    hbwd = _rglru_bwd(z, *rnn, w["wg"][1], w["bg"][1], w["lam"][1], tt=RNN_TT)
    x1 = _rglru_fwd_merge(z, att, hbwd, x, *rnn, w["wg"][0], w["bg"][0], w["lam"][0],
                          w["w_rnn_out"], w["w_att_out"], w["w_out"], tt=RNN_TT)
    u = _norm_matmul(x1.reshape(t, d), w["g_ffn"], w["w_up"], tm=PROJ_TM, tn=PROJ_TN).reshape(b, s, 2 * D_FF)
    return _ffn(u, x1, p, w["ffn_conv_w"], w["ffn_conv_b"], w["w_down"], w["g_ple"], w["w_ple_gate"],
                w["w_ple_proj"], w["g_final"], tt=FFN_TT)


def kernel(x_prompt, x_sample, p_prompt, p_sample, g_mix, w_in, rnn_conv_w, rnn_conv_b, w_rgate, b_rgate,
           w_igate, b_igate, lru_lambda, rpb, w_rnn_out, w_att_out, w_out, g_ffn, w_up, ffn_conv_w,
           ffn_conv_b, w_down, g_ple, w_ple_gate, w_ple_proj, g_final):
    assert g_mix.shape[0] == 1, "single-layer trunk"
    row = lambda v: v.reshape(1, -1)
    w = {
        "g_mix": g_mix[0], "w_in": w_in[0].astype(BF16),
        "rnn_conv_w": rnn_conv_w[0], "rnn_conv_b": row(rnn_conv_b[0]),
        "wg": jnp.concatenate([w_rgate[0], w_igate[0]], axis=-1).astype(BF16),
        "bg": jnp.stack([b_rgate[0], b_igate[0]], axis=1),
        "lam": lru_lambda[0][:, None, :],
        "rpb": rpb[0],
        "w_rnn_out": w_rnn_out[0].astype(BF16), "w_att_out": w_att_out[0].astype(BF16),
        "w_out": w_out[0].astype(BF16),
        "g_ffn": g_ffn[0], "w_up": w_up[0].astype(BF16),
        "ffn_conv_w": ffn_conv_w[0], "ffn_conv_b": row(ffn_conv_b[0]),
        "w_down": w_down[0].astype(BF16), "g_ple": row(g_ple[0]),
        "w_ple_gate": w_ple_gate[0].astype(BF16), "w_ple_proj": w_ple_proj[0].astype(BF16),
        "g_final": row(g_final),
    }
    return _layer(x_prompt, p_prompt[0], w), _layer(x_sample, p_sample[0], w)
```
